```python
import math
import jax, jax.numpy as jnp
from jax import lax
import numpy as np

D_MODEL = 2048
BATCH = 8
SEQ = 2048
DEPTH = 4
DEC_BATCH = 4
DEC_SEQ = 4096
PAST_LEN = 128

D_MIX = D_MODEL
W_A = D_MIX // 4
W_B = D_MIX // 4
W_C = D_MIX // 4
W_D = D_MIX // 4
CHUNK = 128
A_HEADS = 4
A_HEAD_DIM = W_A // A_HEADS
CONV_WIDTH = 31
SHORT_CONV_WIDTH = 3
ATT_HEAD_DIM = 64
Q_HEADS = W_C // ATT_HEAD_DIM
KV_HEADS = 2
Q_PER_KV = Q_HEADS // KV_HEADS
WINDOW = 128
BLOCK = 128
D_IN = 3 * W_A + 3 * W_B + (W_C + 2 * KV_HEADS * ATT_HEAD_DIM + W_C) + 4 * W_D
DEEPNORM_ALPHA = (2.0 * DEPTH) ** 0.25
DEEPNORM_BETA = (8.0 * DEPTH) ** -0.25
LN_EPS = 1e-5
NEG_BIG = -1e30

kernel_name = "hybrid_parallel_group_encoder"


def _layernorm(x, g, b):
    xf = x.astype(jnp.float32)
    mu = xf.mean(-1, keepdims=True)
    var = jnp.square(xf - mu).mean(-1, keepdims=True)
    y = (xf - mu) * lax.rsqrt(var + LN_EPS)
    return (y * g.astype(jnp.float32) + b.astype(jnp.float32)).astype(x.dtype)


def _depthwise_conv(x, w):
    k = w.shape[0]
    return lax.conv_general_dilated(
        x, w[:, None, :].astype(x.dtype), window_strides=(1,),
        padding=((k // 2, k // 2),), dimension_numbers=("NWC", "WIO", "NWC"),
        feature_group_count=x.shape[-1])


def _spatial_gating(u, v, ln_g, ln_b, w_s, b_s):
    bn, s, _ = v.shape
    v = _layernorm(v, ln_g, ln_b)
    v = v.reshape(bn, s // CHUNK, CHUNK, A_HEADS, A_HEAD_DIM)
    mixed = jnp.einsum("hts,bcshd->bcthd", w_s, v) + b_s.T[None, None, :, :, None]
    return u * mixed.reshape(bn, s, W_A)


def _conformer_conv(a, gate, conv_w, conv_b, ln_g, ln_b):
    y = a * jax.nn.sigmoid(gate)
    y = _depthwise_conv(y, conv_w) + conv_b
    y = _layernorm(y, ln_g, ln_b)
    return jax.nn.silu(y)


def _short_gated_conv(bg, cg, xin, conv_w):
    return bg * _depthwise_conv(cg * xin, conv_w)


def _window_attention(q, k, v, sink):
    bn, s, _ = q.shape
    nb = s // BLOCK
    q = q.reshape(bn, nb, BLOCK, KV_HEADS, Q_PER_KV, ATT_HEAD_DIM)
    pad = ((0, 0), (BLOCK, BLOCK), (0, 0))
    k = jnp.pad(k, pad).reshape(bn, nb + 2, BLOCK, KV_HEADS, ATT_HEAD_DIM)
    v = jnp.pad(v, pad).reshape(bn, nb + 2, BLOCK, KV_HEADS, ATT_HEAD_DIM)
    kb = jnp.concatenate([k[:, :-2], k[:, 1:-1], k[:, 2:]], axis=2)
    vb = jnp.concatenate([v[:, :-2], v[:, 1:-1], v[:, 2:]], axis=2)
    scores = jnp.einsum("bnqgrd,bnkgd->bngrqk", q, kb).astype(jnp.float32)
    scores = scores * (ATT_HEAD_DIM ** -0.5)
    qi = jnp.arange(BLOCK)[:, None]
    kj = jnp.arange(3 * BLOCK)[None, :]
    dist = jnp.abs(qi - kj + BLOCK)
    kpos = (jnp.arange(nb)[:, None] - 1) * BLOCK + jnp.arange(3 * BLOCK)[None, :]
    valid = (dist <= WINDOW)[None] & ((kpos >= 0) & (kpos < s))[:, None, :]
    slopes = jnp.exp2(-(8.0 / Q_HEADS) * (jnp.arange(Q_HEADS, dtype=jnp.float32) + 1.0))
    slopes = slopes.reshape(KV_HEADS, Q_PER_KV)
    alibi = -slopes[:, :, None, None] * dist.astype(jnp.float32)[None, None]
    scores = jnp.where(valid[None, :, None, None], scores + alibi[None, None], NEG_BIG)
    sk = sink.astype(jnp.float32).reshape(KV_HEADS, Q_PER_KV)[None, None, :, :, None]
    m = jnp.maximum(scores.max(-1), sk)
    p = jnp.exp(scores - m[..., None])
    denom = p.sum(-1) + jnp.exp(sk - m)
    p = (p / denom[..., None]).astype(vb.dtype)
    out = jnp.einsum("bngrqk,bnkgd->bnqgrd", p, vb)
    return out.reshape(bn, s, W_C)


def _layer(x, w_in, w_out, sgu_ln_g, sgu_ln_b, sgu_w, sgu_b, cm_conv_w, cm_conv_b,
           cm_ln_g, cm_ln_b, attn_sink, sc_conv_w, post_ln_g, post_ln_b):
    h = jnp.einsum("bsd,de->bse", x, w_in)
    kv_w = KV_HEADS * ATT_HEAD_DIM
    sizes = [W_A, W_A, W_A, W_B, W_B, W_B, W_C, kv_w, kv_w, W_C, W_D, W_D, W_D, W_D]
    offs = list(np.cumsum(sizes)[:-1])
    (a_u, a_v, a_g, b_a, b_b, b_g, c_q, c_k, c_v, c_g,
     d_b, d_c, d_x, d_g) = jnp.split(h, offs, axis=-1)
    ya = _spatial_gating(a_u, a_v, sgu_ln_g, sgu_ln_b, sgu_w, sgu_b) * jax.nn.silu(a_g)
    yb = _conformer_conv(b_a, b_b, cm_conv_w, cm_conv_b, cm_ln_g, cm_ln_b) * jax.nn.silu(b_g)
    yc = _window_attention(c_q, c_k, c_v, attn_sink) * jax.nn.silu(c_g)
    yd = _short_gated_conv(d_b, d_c, d_x, sc_conv_w) * jax.nn.silu(d_g)
    y = jnp.concatenate([ya, yb, yc, yd], axis=-1)
    out = jnp.einsum("bse,ed->bsd", y, w_out)
    return _layernorm(DEEPNORM_ALPHA * x + out, post_ln_g, post_ln_b)


def _trunk(x, w_in, w_out, sgu_ln_g, sgu_ln_b, sgu_w, sgu_b, cm_conv_w, cm_conv_b,
           cm_ln_g, cm_ln_b, attn_sink, sc_conv_w, post_ln_g, post_ln_b):
    for l in range(DEPTH):
        x = _layer(x, w_in[l], w_out[l], sgu_ln_g[l], sgu_ln_b[l], sgu_w[l], sgu_b[l],
                   cm_conv_w[l], cm_conv_b[l], cm_ln_g[l], cm_ln_b[l], attn_sink[l],
                   sc_conv_w[l], post_ln_g[l], post_ln_b[l])
    return x


def setup_inputs(seed: int = 0) -> dict:
    key = jax.random.key(seed)
    ks = jax.random.split(key, 18)
    f32 = jnp.float32
    nrm = lambda k, shape, scale: jax.random.normal(k, shape, f32) * scale
    return {
        "x_prompt": nrm(ks[0], (BATCH, SEQ, D_MODEL), 1.0),
        "x_sample": nrm(ks[1], (DEC_BATCH, DEC_SEQ, D_MODEL), 1.0),
        "w_in": nrm(ks[2], (DEPTH, D_MODEL, D_IN), D_MODEL ** -0.5),
        "w_out": nrm(ks[3], (DEPTH, D_MIX, D_MODEL), DEEPNORM_BETA * D_MIX ** -0.5),
        "sgu_ln_g": 1.0 + nrm(ks[4], (DEPTH, W_A), 0.02),
        "sgu_ln_b": nrm(ks[5], (DEPTH, W_A), 0.02),
        "sgu_w": nrm(ks[6], (DEPTH, A_HEADS, CHUNK, CHUNK), CHUNK ** -0.5),
        "sgu_b": 1.0 + nrm(ks[7], (DEPTH, A_HEADS, CHUNK), 0.02),
        "cm_conv_w": nrm(ks[8], (DEPTH, CONV_WIDTH, W_B), CONV_WIDTH ** -0.5),
        "cm_conv_b": nrm(ks[9], (DEPTH, W_B), 0.02),
        "cm_ln_g": 1.0 + nrm(ks[10], (DEPTH, W_B), 0.02),
        "cm_ln_b": nrm(ks[11], (DEPTH, W_B), 0.02),
        "attn_sink": nrm(ks[12], (DEPTH, Q_HEADS), 0.5),
        "sc_conv_w": nrm(ks[13], (DEPTH, SHORT_CONV_WIDTH, W_D), SHORT_CONV_WIDTH ** -0.5),
        "post_ln_g": 1.0 + nrm(ks[14], (DEPTH, D_MODEL), 0.02),
        "post_ln_b": nrm(ks[15], (DEPTH, D_MODEL), 0.02),
    }


def reference(x_prompt, x_sample, w_in, w_out, sgu_ln_g, sgu_ln_b, sgu_w, sgu_b,
              cm_conv_w, cm_conv_b, cm_ln_g, cm_ln_b, attn_sink, sc_conv_w,
              post_ln_g, post_ln_b):
    y_prompt = _trunk(x_prompt, w_in, w_out, sgu_ln_g, sgu_ln_b, sgu_w, sgu_b,
                      cm_conv_w, cm_conv_b, cm_ln_g, cm_ln_b, attn_sink, sc_conv_w,
                      post_ln_g, post_ln_b)
    y_sample = _trunk(x_sample, w_in, w_out, sgu_ln_g, sgu_ln_b, sgu_w, sgu_b,
                      cm_conv_w, cm_conv_b, cm_ln_g, cm_ln_b, attn_sink, sc_conv_w,
                      post_ln_g, post_ln_b)
    return (y_prompt, y_sample)
```

```python
import functools
import math

import jax
import jax.numpy as jnp
from jax import lax
from jax.experimental import pallas as pl
from jax.experimental.pallas import tpu as pltpu

D_MODEL = 2048
DEPTH = 4
GROUP_W = 512
CHUNK = 128
A_HEADS = 4
CONV_WIDTH = 31
CONV_HALF = CONV_WIDTH // 2
SHORT_CONV_WIDTH = 3
ATT_HEAD_DIM = 64
Q_HEADS = 8
KV_HEADS = 2
Q_PER_KV = Q_HEADS // KV_HEADS
WINDOW = 128
KV_W = KV_HEADS * ATT_HEAD_DIM
D_IN = 6400
DEEPNORM_ALPHA = (2.0 * DEPTH) ** 0.25
LN_EPS = 1e-5
NEG_BIG = -1e30

OFF_A = 0
OFF_B = 3 * GROUP_W
OFF_C = 6 * GROUP_W
OFF_CK = OFF_C + GROUP_W
OFF_CG = OFF_CK + 2 * KV_W
OFF_D = OFF_CG + GROUP_W
assert OFF_D + 4 * GROUP_W == D_IN

LANES = 128
BF16_SUBLANES = 16
V7X_VMEM_BYTES = 64 * 1024 * 1024

TILE = 256
HALO = WINDOW
CONV_HALO = BF16_SUBLANES
assert CONV_HALO >= CONV_HALF and HALO % CONV_HALO == 0


def _layernorm(x, g, b):
    mu = jnp.mean(x, axis=-1, keepdims=True)
    xc = x - mu
    var = jnp.mean(xc * xc, axis=-1, keepdims=True)
    return xc * lax.rsqrt(var + LN_EPS) * g + b


def _silu(x):
    return x * jax.nn.sigmoid(x)


def _mm(a, b):
    return jnp.dot(a, b, preferred_element_type=jnp.float32)


def _layer_kernel(sink_ref, xl_ref, xm_ref, xr_ref, w_in_ref, w_out_ref,
                  sgu_ln_g_ref, sgu_ln_b_ref, sgu_w_ref, sgu_b_ref,
                  cm_conv_w_ref, cm_conv_b_ref, cm_ln_g_ref, cm_ln_b_ref,
                  sc_conv_w_ref, post_ln_g_ref, post_ln_b_ref,
                  out_ref, xb_ref, y_ref, cbuf_ref, dbuf_ref, *, seq_len):
    tile = xm_ref.shape[1]
    i = pl.program_id(1)
    t0 = i * tile
    bf16 = jnp.bfloat16

    xb_ref[0:HALO, :] = xl_ref[0].astype(bf16)
    xb_ref[HALO:HALO + tile, :] = xm_ref[0].astype(bf16)
    xb_ref[HALO + tile:, :] = xr_ref[0].astype(bf16)

    main = slice(HALO, HALO + tile)
    conv_rows = slice(HALO - CONV_HALO, HALO + tile + CONV_HALO)
    n_conv = tile + 2 * CONV_HALO

    def proj(rows, off, width):
        return _mm(xb_ref[rows, :], w_in_ref[:, off:off + width])

    cpos = t0 - CONV_HALO + lax.broadcasted_iota(jnp.int32, (n_conv, 1), 0)
    conv_valid = (cpos >= 0) & (cpos < seq_len)

    h_a = proj(main, OFF_A, 3 * GROUP_W)
    a_u = h_a[:, 0:GROUP_W]
    a_v = _layernorm(h_a[:, GROUP_W:2 * GROUP_W], sgu_ln_g_ref[...], sgu_ln_b_ref[...]).astype(bf16)
    a_gate = _silu(h_a[:, 2 * GROUP_W:3 * GROUP_W])
    for c in range(tile // CHUNK):
        rows = slice(c * CHUNK, (c + 1) * CHUNK)
        for h in range(A_HEADS):
            cols = slice(h * LANES, (h + 1) * LANES)
            mixed = _mm(sgu_w_ref[h], a_v[rows, cols]) + sgu_b_ref[h]
            y_ref[rows, h * LANES:(h + 1) * LANES] = (
                a_u[rows, cols] * mixed * a_gate[rows, cols]).astype(bf16)

    h_b = proj(conv_rows, OFF_B, 2 * GROUP_W)
    glu = h_b[:, 0:GROUP_W] * jax.nn.sigmoid(h_b[:, GROUP_W:2 * GROUP_W])
    cbuf_ref[...] = jnp.where(conv_valid, glu, 0.0)
    base = CONV_HALO - CONV_HALF
    conv = cbuf_ref[base:base + tile, :] * cm_conv_w_ref[0:1, :]
    for k in range(1, CONV_WIDTH):
        conv = conv + cbuf_ref[base + k:base + k + tile, :] * cm_conv_w_ref[k:k + 1, :]
    conv = conv + cm_conv_b_ref[...]
    b_act = _silu(_layernorm(conv, cm_ln_g_ref[...], cm_ln_b_ref[...]))
    b_gate = _silu(proj(main, OFF_B + 2 * GROUP_W, GROUP_W))
    y_ref[:, GROUP_W:2 * GROUP_W] = (b_act * b_gate).astype(bf16)

    q = (proj(main, OFF_C, GROUP_W) * (ATT_HEAD_DIM ** -0.5)).astype(bf16)
    kv = proj(slice(0, tile + 2 * HALO), OFF_CK, 2 * KV_W)
    c_gate = _silu(proj(main, OFF_CG, GROUP_W))
    k_all = kv[:, 0:KV_W]
    v_all = kv[:, KV_W:2 * KV_W]
    lane = lax.broadcasted_iota(jnp.int32, (1, LANES), 1)
    low_half = lane < ATT_HEAD_DIM
    k_sw = pltpu.roll(k_all, ATT_HEAD_DIM, 1)
    v_sw = pltpu.roll(v_all, ATT_HEAD_DIM, 1)
    kk = [jnp.where(low_half, k_all, k_sw).astype(bf16), jnp.where(low_half, k_sw, k_all).astype(bf16)]
    vv = [jnp.where(low_half, v_all, v_sw).astype(bf16), jnp.where(low_half, v_sw, v_all).astype(bf16)]

    n_keys = 3 * WINDOW
    qi = lax.broadcasted_iota(jnp.int32, (WINDOW, n_keys), 0)
    kj = lax.broadcasted_iota(jnp.int32, (WINDOW, n_keys), 1)
    dist = jnp.abs(qi - kj + WINDOW)
    in_window = dist <= WINDOW
    dist_f = dist.astype(jnp.float32)
    zero_q = jnp.zeros((), bf16)
    for n in range(tile // WINDOW):
        rows = slice(n * WINDOW, (n + 1) * WINDOW)
        krows = slice(n * WINDOW, n * WINDOW + n_keys)
        kpos = t0 + (n - 1) * WINDOW + kj
        valid = in_window & (kpos >= 0) & (kpos < seq_len)
        for g in range(KV_HEADS):
            stacked = []
            for r in range(Q_PER_KV):
                head = g * Q_PER_KV + r
                q_pair = q[rows, (head // 2) * LANES:(head // 2 + 1) * LANES]
                keep = low_half if head % 2 == 0 else ~low_half
                stacked.append(jnp.where(keep, q_pair, zero_q))
            q_stack = jnp.concatenate(stacked, axis=0)
            scores = lax.dot_general(q_stack, kk[g][krows, :], (((1,), (1,)), ((), ())),
                                     preferred_element_type=jnp.float32)
            probs = []
            for r in range(Q_PER_KV):
                head = g * Q_PER_KV + r
                slope = 2.0 ** (-(8.0 / Q_HEADS) * (head + 1.0))
                sink = sink_ref[head]
                s = scores[r * WINDOW:(r + 1) * WINDOW, :] - slope * dist_f
                s = jnp.where(valid, s, NEG_BIG)
                m = jnp.maximum(jnp.max(s, axis=-1, keepdims=True), sink)
                p = jnp.exp(s - m)
                denom = jnp.sum(p, axis=-1, keepdims=True) + jnp.exp(sink - m)
                probs.append((p / denom).astype(bf16))
            p_stack = jnp.concatenate(probs, axis=0)
            o = _mm(p_stack, vv[g][krows, :])
            for j in range(Q_PER_KV // 2):
                even = o[(2 * j) * WINDOW:(2 * j + 1) * WINDOW, :]
                odd = o[(2 * j + 1) * WINDOW:(2 * j + 2) * WINDOW, :]
                grp = g * (Q_PER_KV // 2) + j
                cols = slice(grp * LANES, (grp + 1) * LANES)
                y_ref[rows, 2 * GROUP_W + grp * LANES:2 * GROUP_W + (grp + 1) * LANES] = (
                    jnp.where(low_half, even, odd) * c_gate[rows, cols]).astype(bf16)

    d_b = proj(main, OFF_D, GROUP_W)
    h_d = proj(conv_rows, OFF_D + GROUP_W, 2 * GROUP_W)
    dbuf_ref[...] = jnp.where(conv_valid, h_d[:, 0:GROUP_W] * h_d[:, GROUP_W:2 * GROUP_W], 0.0)
    base = CONV_HALO - SHORT_CONV_WIDTH // 2
    sconv = dbuf_ref[base:base + tile, :] * sc_conv_w_ref[0:1, :]
    for k in range(1, SHORT_CONV_WIDTH):
        sconv = sconv + dbuf_ref[base + k:base + k + tile, :] * sc_conv_w_ref[k:k + 1, :]
    d_gate = _silu(proj(main, OFF_D + 3 * GROUP_W, GROUP_W))
    y_ref[:, 3 * GROUP_W:4 * GROUP_W] = (d_b * sconv * d_gate).astype(bf16)

    out = _mm(y_ref[...], w_out_ref[...])
    out_ref[0] = _layernorm(DEEPNORM_ALPHA * xm_ref[0] + out, post_ln_g_ref[...], post_ln_b_ref[...])


def _resident(shape):
    return pl.BlockSpec(shape, lambda b, i: (0,) * len(shape), pipeline_mode=pl.Buffered(1))


def _layer(x, sink, w_in, w_out, sgu_ln_g, sgu_ln_b, sgu_w, sgu_b, cm_conv_w, cm_conv_b,
           cm_ln_g, cm_ln_b, sc_conv_w, post_ln_g, post_ln_b):
    batch, seq_len, d = x.shape
    tile = TILE
    assert d == D_MODEL and seq_len % tile == 0 and tile % HALO == 0
    halo_per_tile = tile // HALO
    n_halo_blocks = seq_len // HALO

    in_specs = [
        pl.BlockSpec(memory_space=pltpu.SMEM),
        pl.BlockSpec((1, HALO, d), lambda b, i: (b, jnp.maximum(i * halo_per_tile - 1, 0), 0)),
        pl.BlockSpec((1, tile, d), lambda b, i: (b, i, 0)),
        pl.BlockSpec((1, HALO, d), lambda b, i: (b, jnp.minimum((i + 1) * halo_per_tile, n_halo_blocks - 1), 0)),
        _resident((D_MODEL, D_IN)),
        _resident((D_MODEL, D_MODEL)),
        _resident((1, GROUP_W)), _resident((1, GROUP_W)),
        _resident((A_HEADS, CHUNK, CHUNK)), _resident((A_HEADS, CHUNK, 1)),
        _resident((CONV_WIDTH, GROUP_W)), _resident((1, GROUP_W)),
        _resident((1, GROUP_W)), _resident((1, GROUP_W)),
        _resident((SHORT_CONV_WIDTH, GROUP_W)),
        _resident((1, D_MODEL)), _resident((1, D_MODEL)),
    ]
    scratch = [
        pltpu.VMEM((tile + 2 * HALO, D_MODEL), jnp.bfloat16),
        pltpu.VMEM((tile, D_MODEL), jnp.bfloat16),
        pltpu.VMEM((tile + 2 * CONV_HALO, GROUP_W), jnp.float32),
        pltpu.VMEM((tile + 2 * CONV_HALO, GROUP_W), jnp.float32),
    ]
    return pl.pallas_call(
        functools.partial(_layer_kernel, seq_len=seq_len),
        grid=(batch, seq_len // tile),
        in_specs=in_specs,
        out_specs=pl.BlockSpec((1, tile, d), lambda b, i: (b, i, 0)),
        out_shape=jax.ShapeDtypeStruct(x.shape, x.dtype),
        scratch_shapes=scratch,
        compiler_params=pltpu.CompilerParams(
            dimension_semantics=("arbitrary", "arbitrary"),
            vmem_limit_bytes=V7X_VMEM_BYTES - 4 * 1024 * 1024),
        name="encoder_layer",
    )(sink, x, x, x, w_in, w_out, sgu_ln_g, sgu_ln_b, sgu_w, sgu_b, cm_conv_w, cm_conv_b,
      cm_ln_g, cm_ln_b, sc_conv_w, post_ln_g, post_ln_b)


def kernel(x_prompt, x_sample, w_in, w_out, sgu_ln_g, sgu_ln_b, sgu_w, sgu_b, cm_conv_w, cm_conv_b,
           cm_ln_g, cm_ln_b, attn_sink, sc_conv_w, post_ln_g, post_ln_b):
    bf16 = jnp.bfloat16
    w_in_b = w_in.astype(bf16)
    w_out_b = w_out.astype(bf16)
    sgu_w_b = sgu_w.astype(bf16)
    row = lambda a: a[:, None, :]

    def trunk(x):
        for l in range(DEPTH):
            x = _layer(x, attn_sink[l], w_in_b[l], w_out_b[l], row(sgu_ln_g)[l], row(sgu_ln_b)[l],
                       sgu_w_b[l], sgu_b[l][:, :, None], cm_conv_w[l], row(cm_conv_b)[l],
                       row(cm_ln_g)[l], row(cm_ln_b)[l], sc_conv_w[l], row(post_ln_g)[l], row(post_ln_b)[l])
        return x

    return (trunk(x_prompt), trunk(x_sample))
```

```python
import functools
import math

import jax
import jax.numpy as jnp
from jax import lax
from jax.experimental import pallas as pl
from jax.experimental.pallas import tpu as pltpu

D_MODEL = 2048
DEPTH = 4
GROUP_W = 512
CHUNK = 128
A_HEADS = 4
CONV_WIDTH = 31
CONV_HALF = CONV_WIDTH // 2
SHORT_CONV_WIDTH = 3
ATT_HEAD_DIM = 64
Q_HEADS = 8
KV_HEADS = 2
Q_PER_KV = Q_HEADS // KV_HEADS
WINDOW = 128
KV_W = KV_HEADS * ATT_HEAD_DIM
D_IN = 6400
DEEPNORM_ALPHA = (2.0 * DEPTH) ** 0.25
LN_EPS = 1e-5
NEG_BIG = -1e30

OFF_A = 0
OFF_B = 3 * GROUP_W
OFF_C = 6 * GROUP_W
OFF_CK = OFF_C + GROUP_W
OFF_CG = OFF_CK + 2 * KV_W
OFF_D = OFF_CG + GROUP_W
assert OFF_D + 4 * GROUP_W == D_IN

LANES = 128
SUBLANES = 8
BF16_SUBLANES = 16
CONV_ROWS = 64
V7X_VMEM_BYTES = 64 * 1024 * 1024

TILE = 256
HALO = WINDOW
CONV_HALO = BF16_SUBLANES
assert CONV_HALO >= CONV_HALF and HALO % CONV_HALO == 0


def _layernorm(x, g, b):
    mu = jnp.mean(x, axis=-1, keepdims=True)
    xc = x - mu
    var = jnp.mean(xc * xc, axis=-1, keepdims=True)
    return xc * lax.rsqrt(var + LN_EPS) * g + b


def _silu(x):
    return x * jax.nn.sigmoid(x)


def _mm(a, b):
    return jnp.dot(a, b, preferred_element_type=jnp.float32)


def _depthwise_conv(buf_ref, w_ref, n_taps, first_row, row0, n_rows):
    out = None
    for s in range(SUBLANES):
        taps = [k for k in range(n_taps) if (first_row + k) % SUBLANES == s]
        if not taps:
            continue
        part = None
        for k in taps:
            a = row0 + first_row + k - s
            term = buf_ref[a:a + n_rows + SUBLANES, :] * w_ref[k:k + 1, :]
            part = term if part is None else part + term
        shifted = part[s:s + n_rows, :]
        out = shifted if out is None else out + shifted
    return out


def _layer_kernel(sink_ref, xl_ref, xm_ref, xr_ref, w_in_ref, w_out_ref,
                  sgu_ln_g_ref, sgu_ln_b_ref, sgu_w_ref, sgu_b_ref,
                  cm_conv_w_ref, cm_conv_b_ref, cm_ln_g_ref, cm_ln_b_ref,
                  sc_conv_w_ref, post_ln_g_ref, post_ln_b_ref,
                  out_ref, xb_ref, y_ref, cbuf_ref, dbuf_ref, *, layer, seq_len):
    tile = xm_ref.shape[1]
    i = pl.program_id(1)
    t0 = i * tile
    bf16 = jnp.bfloat16

    xb_ref[0:HALO, :] = xl_ref[0].astype(bf16)
    xb_ref[HALO:HALO + tile, :] = xm_ref[0].astype(bf16)
    xb_ref[HALO + tile:, :] = xr_ref[0].astype(bf16)

    main = slice(HALO, HALO + tile)
    conv_rows = slice(HALO - CONV_HALO, HALO + tile + CONV_HALO)
    n_conv = tile + 2 * CONV_HALO

    def proj(rows, off, width):
        return _mm(xb_ref[rows, :], w_in_ref[:, off:off + width])

    cpos = t0 - CONV_HALO + lax.broadcasted_iota(jnp.int32, (n_conv, 1), 0)
    conv_valid = (cpos >= 0) & (cpos < seq_len)

    h_a = proj(main, OFF_A, 3 * GROUP_W)
    a_u = h_a[:, 0:GROUP_W]
    a_v = _layernorm(h_a[:, GROUP_W:2 * GROUP_W], sgu_ln_g_ref[...], sgu_ln_b_ref[...]).astype(bf16)
    a_gate = _silu(h_a[:, 2 * GROUP_W:3 * GROUP_W])
    for c in range(tile // CHUNK):
        rows = slice(c * CHUNK, (c + 1) * CHUNK)
        for h in range(A_HEADS):
            cols = slice(h * LANES, (h + 1) * LANES)
            mixed = _mm(sgu_w_ref[h], a_v[rows, cols]) + sgu_b_ref[h]
            y_ref[rows, h * LANES:(h + 1) * LANES] = (
                a_u[rows, cols] * mixed * a_gate[rows, cols]).astype(bf16)

    h_b = proj(conv_rows, OFF_B, 2 * GROUP_W)
    glu = h_b[:, 0:GROUP_W] * jax.nn.sigmoid(h_b[:, GROUP_W:2 * GROUP_W])
    cbuf_ref[...] = jnp.where(conv_valid, glu, 0.0)
    b_gate = _silu(proj(main, OFF_B + 2 * GROUP_W, GROUP_W))
    for r0 in range(0, tile, CONV_ROWS):
        conv = _depthwise_conv(cbuf_ref, cm_conv_w_ref, CONV_WIDTH, CONV_HALO - CONV_HALF, r0, CONV_ROWS)
        conv = conv + cm_conv_b_ref[...]
        b_act = _silu(_layernorm(conv, cm_ln_g_ref[...], cm_ln_b_ref[...]))
        y_ref[r0:r0 + CONV_ROWS, GROUP_W:2 * GROUP_W] = (b_act * b_gate[r0:r0 + CONV_ROWS, :]).astype(bf16)

    q = (proj(main, OFF_C, GROUP_W) * (ATT_HEAD_DIM ** -0.5)).astype(bf16)
    kv = proj(slice(0, tile + 2 * HALO), OFF_CK, 2 * KV_W)
    c_gate = _silu(proj(main, OFF_CG, GROUP_W))
    k_all = kv[:, 0:KV_W]
    v_all = kv[:, KV_W:2 * KV_W]
    lane = lax.broadcasted_iota(jnp.int32, (1, LANES), 1)
    low_half = lane < ATT_HEAD_DIM
    k_sw = pltpu.roll(k_all, ATT_HEAD_DIM, 1)
    v_sw = pltpu.roll(v_all, ATT_HEAD_DIM, 1)
    kk = [jnp.where(low_half, k_all, k_sw).astype(bf16), jnp.where(low_half, k_sw, k_all).astype(bf16)]
    vv = [jnp.where(low_half, v_all, v_sw).astype(bf16), jnp.where(low_half, v_sw, v_all).astype(bf16)]

    n_keys = 3 * WINDOW
    qi = lax.broadcasted_iota(jnp.int32, (WINDOW, n_keys), 0)
    kj = lax.broadcasted_iota(jnp.int32, (WINDOW, n_keys), 1)
    dist = jnp.abs(qi - kj + WINDOW)
    in_window = dist <= WINDOW
    dist_f = dist.astype(jnp.float32)
    zero_q = jnp.zeros((), bf16)
    for n in range(tile // WINDOW):
        rows = slice(n * WINDOW, (n + 1) * WINDOW)
        krows = slice(n * WINDOW, n * WINDOW + n_keys)
        kpos = t0 + (n - 1) * WINDOW + kj
        valid = in_window & (kpos >= 0) & (kpos < seq_len)
        for g in range(KV_HEADS):
            stacked = []
            for r in range(Q_PER_KV):
                head = g * Q_PER_KV + r
                q_pair = q[rows, (head // 2) * LANES:(head // 2 + 1) * LANES]
                keep = low_half if head % 2 == 0 else ~low_half
                stacked.append(jnp.where(keep, q_pair, zero_q))
            q_stack = jnp.concatenate(stacked, axis=0)
            scores = lax.dot_general(q_stack, kk[g][krows, :], (((1,), (1,)), ((), ())),
                                     preferred_element_type=jnp.float32)
            probs = []
            for r in range(Q_PER_KV):
                head = g * Q_PER_KV + r
                slope = 2.0 ** (-(8.0 / Q_HEADS) * (head + 1.0))
                sink = sink_ref[layer, head]
                s = scores[r * WINDOW:(r + 1) * WINDOW, :] - slope * dist_f
                s = jnp.where(valid, s, NEG_BIG)
                m = jnp.maximum(jnp.max(s, axis=-1, keepdims=True), sink)
                p = jnp.exp(s - m)
                denom = jnp.sum(p, axis=-1, keepdims=True) + jnp.exp(sink - m)
                probs.append((p / denom).astype(bf16))
            p_stack = jnp.concatenate(probs, axis=0)
            o = _mm(p_stack, vv[g][krows, :])
            for j in range(Q_PER_KV // 2):
                even = o[(2 * j) * WINDOW:(2 * j + 1) * WINDOW, :]
                odd = o[(2 * j + 1) * WINDOW:(2 * j + 2) * WINDOW, :]
                grp = g * (Q_PER_KV // 2) + j
                cols = slice(grp * LANES, (grp + 1) * LANES)
                y_ref[rows, 2 * GROUP_W + grp * LANES:2 * GROUP_W + (grp + 1) * LANES] = (
                    jnp.where(low_half, even, odd) * c_gate[rows, cols]).astype(bf16)

    d_b = proj(main, OFF_D, GROUP_W)
    h_d = proj(conv_rows, OFF_D + GROUP_W, 2 * GROUP_W)
    dbuf_ref[...] = jnp.where(conv_valid, h_d[:, 0:GROUP_W] * h_d[:, GROUP_W:2 * GROUP_W], 0.0)
    sconv = _depthwise_conv(dbuf_ref, sc_conv_w_ref, SHORT_CONV_WIDTH,
                            CONV_HALO - SHORT_CONV_WIDTH // 2, 0, tile)
    d_gate = _silu(proj(main, OFF_D + 3 * GROUP_W, GROUP_W))
    y_ref[:, 3 * GROUP_W:4 * GROUP_W] = (d_b * sconv * d_gate).astype(bf16)

    out = _mm(y_ref[...], w_out_ref[...])
    out_ref[0] = _layernorm(DEEPNORM_ALPHA * xm_ref[0] + out, post_ln_g_ref[...], post_ln_b_ref[...])


def _layer_params(layer, shape):
    return pl.BlockSpec((None,) + shape, lambda b, i: (layer,) + (0,) * len(shape),
                        pipeline_mode=pl.Buffered(1))


def _layer(layer, x, sink, w_in, w_out, sgu_ln_g, sgu_ln_b, sgu_w, sgu_b, cm_conv_w, cm_conv_b,
           cm_ln_g, cm_ln_b, sc_conv_w, post_ln_g, post_ln_b):
    _resident = functools.partial(_layer_params, layer)
    batch, seq_len, d = x.shape
    tile = TILE
    assert d == D_MODEL and seq_len % tile == 0 and tile % HALO == 0
    halo_per_tile = tile // HALO
    n_halo_blocks = seq_len // HALO

    in_specs = [
        pl.BlockSpec(memory_space=pltpu.SMEM),
        pl.BlockSpec((1, HALO, d), lambda b, i: (b, jnp.maximum(i * halo_per_tile - 1, 0), 0)),
        pl.BlockSpec((1, tile, d), lambda b, i: (b, i, 0)),
        pl.BlockSpec((1, HALO, d), lambda b, i: (b, jnp.minimum((i + 1) * halo_per_tile, n_halo_blocks - 1), 0)),
        _resident((D_MODEL, D_IN)),
        _resident((D_MODEL, D_MODEL)),
        _resident((1, GROUP_W)), _resident((1, GROUP_W)),
        _resident((A_HEADS, CHUNK, CHUNK)), _resident((A_HEADS, CHUNK, 1)),
        _resident((CONV_WIDTH, GROUP_W)), _resident((1, GROUP_W)),
        _resident((1, GROUP_W)), _resident((1, GROUP_W)),
        _resident((SHORT_CONV_WIDTH, GROUP_W)),
        _resident((1, D_MODEL)), _resident((1, D_MODEL)),
    ]
    scratch = [
        pltpu.VMEM((tile + 2 * HALO, D_MODEL), jnp.bfloat16),
        pltpu.VMEM((tile, D_MODEL), jnp.bfloat16),
        pltpu.VMEM((tile + 2 * CONV_HALO, GROUP_W), jnp.float32),
        pltpu.VMEM((tile + 2 * CONV_HALO, GROUP_W), jnp.float32),
    ]
    return pl.pallas_call(
        functools.partial(_layer_kernel, layer=layer, seq_len=seq_len),
        grid=(batch, seq_len // tile),
        in_specs=in_specs,
        out_specs=pl.BlockSpec((1, tile, d), lambda b, i: (b, i, 0)),
        out_shape=jax.ShapeDtypeStruct(x.shape, x.dtype),
        scratch_shapes=scratch,
        compiler_params=pltpu.CompilerParams(
            dimension_semantics=("arbitrary", "arbitrary"),
            vmem_limit_bytes=V7X_VMEM_BYTES - 4 * 1024 * 1024),
        name="encoder_layer",
    )(sink, x, x, x, w_in, w_out, sgu_ln_g, sgu_ln_b, sgu_w, sgu_b, cm_conv_w, cm_conv_b,
      cm_ln_g, cm_ln_b, sc_conv_w, post_ln_g, post_ln_b)


def kernel(x_prompt, x_sample, w_in, w_out, sgu_ln_g, sgu_ln_b, sgu_w, sgu_b, cm_conv_w, cm_conv_b,
           cm_ln_g, cm_ln_b, attn_sink, sc_conv_w, post_ln_g, post_ln_b):
    bf16 = jnp.bfloat16
    row = lambda a: a[:, None, :]
    params = (attn_sink, w_in.astype(bf16), w_out.astype(bf16), row(sgu_ln_g), row(sgu_ln_b),
              sgu_w.astype(bf16), sgu_b[:, :, :, None], cm_conv_w, row(cm_conv_b),
              row(cm_ln_g), row(cm_ln_b), sc_conv_w, row(post_ln_g), row(post_ln_b))

    def trunk(x):
        for layer in range(DEPTH):
            x = _layer(layer, x, *params)
        return x

    return (trunk(x_prompt), trunk(x_sample))
```

```python
import functools
import math

import jax
import jax.numpy as jnp
from jax import lax
from jax.experimental import pallas as pl
from jax.experimental.pallas import tpu as pltpu

D_MODEL = 2048
DEPTH = 4
GROUP_W = 512
CHUNK = 128
A_HEADS = 4
CONV_WIDTH = 31
CONV_HALF = CONV_WIDTH // 2
SHORT_CONV_WIDTH = 3
ATT_HEAD_DIM = 64
Q_HEADS = 8
KV_HEADS = 2
Q_PER_KV = Q_HEADS // KV_HEADS
WINDOW = 128
KV_W = KV_HEADS * ATT_HEAD_DIM
D_IN = 6400
DEEPNORM_ALPHA = (2.0 * DEPTH) ** 0.25
LN_EPS = 1e-5
NEG_BIG = -1e30

OFF_A = 0
OFF_B = 3 * GROUP_W
OFF_C = 6 * GROUP_W
OFF_CK = OFF_C + GROUP_W
OFF_CG = OFF_CK + 2 * KV_W
OFF_D = OFF_CG + GROUP_W
assert OFF_D + 4 * GROUP_W == D_IN

LANES = 128
SUBLANES = 8
BF16_SUBLANES = 16
CONV_ROWS = 64
V7X_VMEM_BYTES = 64 * 1024 * 1024

TILE = 256
HALO = WINDOW
CONV_HALO = BF16_SUBLANES
assert CONV_HALO >= CONV_HALF and HALO % CONV_HALO == 0


def _layernorm(x, g, b):
    mu = jnp.mean(x, axis=-1, keepdims=True)
    xc = x - mu
    var = jnp.mean(xc * xc, axis=-1, keepdims=True)
    return xc * lax.rsqrt(var + LN_EPS) * g + b


def _silu(x):
    return x * jax.nn.sigmoid(x)


def _mm(a, b):
    return jnp.dot(a, b, preferred_element_type=jnp.float32)


def _dep_zero(tile):
    u = lax.bitcast_convert_type(tile, jnp.uint32)
    return lax.shift_right_logical(lax.shift_right_logical(u, jnp.uint32(16)), jnp.uint32(16))


def _after(tile, *deps):
    u = lax.bitcast_convert_type(tile, jnp.uint32)
    for d in deps:
        u = u | _dep_zero(d)
    return lax.bitcast_convert_type(u, jnp.float32)


def _value_after(x, token):
    if token is None:
        return x
    head = x[0:SUBLANES, :]
    first = _after(head[:, 0:LANES], token)
    if head.shape[1] > LANES:
        first = jnp.concatenate([first, head[:, LANES:]], axis=1)
    if x.shape[0] == SUBLANES:
        return first
    return jnp.concatenate([first, x[SUBLANES:, :]], axis=0)


def _depthwise_conv(buf_ref, w_ref, n_taps, first_row, row0, n_rows, cols=slice(None), start_after=None):
    out = None
    for s in range(SUBLANES):
        taps = [k for k in range(n_taps) if (first_row + k) % SUBLANES == s]
        if not taps:
            continue
        part = None
        for k in taps:
            a = row0 + first_row + k - s
            blk = buf_ref[a:a + n_rows + SUBLANES, cols]
            if part is None and out is None:
                blk = _value_after(blk, start_after)
            term = blk * w_ref[k:k + 1, cols]
            part = term if part is None else part + term
        shifted = part[s:s + n_rows, :]
        out = shifted if out is None else out + shifted
    return out


def _layer_kernel(sink_ref, xl_ref, xm_ref, xr_ref, w_in_ref, w_out_ref,
                  sgu_ln_g_ref, sgu_ln_b_ref, sgu_w_ref, sgu_b_ref,
                  cm_conv_w_ref, cm_conv_b_ref, cm_ln_g_ref, cm_ln_b_ref,
                  sc_conv_w_ref, post_ln_g_ref, post_ln_b_ref,
                  out_ref, xb_ref, y_ref, cbuf_ref, dbuf_ref, convout_ref, wide_ref, ha_ref, av_ref,
                  bgate_ref, q_ref, kk_ref, vv_ref, cg_ref, db_ref, dg_ref, *, layer, seq_len):
    tile = xm_ref.shape[1]
    i = pl.program_id(1)
    t0 = i * tile
    bf16 = jnp.bfloat16

    xb_ref[0:HALO, :] = xl_ref[0].astype(bf16)
    xb_ref[HALO:HALO + tile, :] = xm_ref[0].astype(bf16)
    xb_ref[HALO + tile:, :] = xr_ref[0].astype(bf16)

    main = slice(HALO, HALO + tile)
    conv_rows = slice(HALO - CONV_HALO, HALO + tile + CONV_HALO)
    kv_rows = slice(0, tile + 2 * HALO)
    n_conv = tile + 2 * CONV_HALO

    cpos = t0 - CONV_HALO + lax.broadcasted_iota(jnp.int32, (n_conv, 1), 0)
    conv_valid = (cpos >= 0) & (cpos < seq_len)
    lane = lax.broadcasted_iota(jnp.int32, (1, LANES), 1)
    low_half = lane < ATT_HEAD_DIM

    state = {"token": None}

    def unit(store, rows, off, width, pieces=()):
        started = [p[0](state["token"]) if isinstance(p, tuple) else p(state["token"]) for p in pieces]
        r = _mm(xb_ref[rows, :], w_in_ref[:, off:off + width])
        done = [p[1](st) if isinstance(p, tuple) else st for p, st in zip(pieces, started)]
        head = r[0:SUBLANES, 0:LANES]
        if done:
            head = _after(head, *done)
            first = jnp.concatenate([head, r[0:SUBLANES, LANES:]], axis=1)
            r = jnp.concatenate([first, r[SUBLANES:, :]], axis=0)
        store(r)
        state["token"] = r[SUBLANES:2 * SUBLANES, 0:LANES]

    def kv_prep(token):
        k_all = _value_after(wide_ref[0:tile + 2 * HALO, 0:KV_W], token)
        v_all = wide_ref[0:tile + 2 * HALO, KV_W:2 * KV_W]
        k_sw = pltpu.roll(k_all, ATT_HEAD_DIM, 1)
        v_sw = pltpu.roll(v_all, ATT_HEAD_DIM, 1)
        kk_ref[0] = jnp.where(low_half, k_all, k_sw).astype(bf16)
        kk_ref[1] = jnp.where(low_half, k_sw, k_all).astype(bf16)
        vv_ref[0] = jnp.where(low_half, v_all, v_sw).astype(bf16)
        vv_ref[1] = jnp.where(low_half, v_sw, v_all).astype(bf16)
        return k_sw[0:SUBLANES, :]

    n_keys = 3 * WINDOW

    def attention(n, g):
        rows = slice(n * WINDOW, (n + 1) * WINDOW)
        krows = slice(n * WINDOW, n * WINDOW + n_keys)

        def head(token):
            qi = lax.broadcasted_iota(jnp.int32, (WINDOW, n_keys), 0)
            kj = lax.broadcasted_iota(jnp.int32, (WINDOW, n_keys), 1)
            dist = jnp.abs(qi - kj + WINDOW)
            dist_f = dist.astype(jnp.float32)
            kpos = t0 + (n - 1) * WINDOW + kj
            valid = (dist <= WINDOW) & (kpos >= 0) & (kpos < seq_len)
            stacked = []
            for r in range(Q_PER_KV):
                hd = g * Q_PER_KV + r
                q_pair = q_ref[rows, (hd // 2) * LANES:(hd // 2 + 1) * LANES]
                keep = low_half if hd % 2 == 0 else ~low_half
                stacked.append(jnp.where(keep, q_pair, jnp.zeros((), bf16)))
            q_stack = jnp.concatenate(stacked, axis=0)
            scores = lax.dot_general(q_stack, kk_ref[g, krows, :], (((1,), (1,)), ((), ())),
                                     preferred_element_type=jnp.float32)
            probs = []
            for r in range(Q_PER_KV):
                hd = g * Q_PER_KV + r
                slope = 2.0 ** (-(8.0 / Q_HEADS) * (hd + 1.0))
                sink = sink_ref[layer, hd]
                s = scores[r * WINDOW:(r + 1) * WINDOW, :]
                if r == 0:
                    s = _value_after(s, token)
                s = jnp.where(valid, s - slope * dist_f, NEG_BIG)
                m = jnp.maximum(jnp.max(s, axis=-1, keepdims=True), sink)
                p = jnp.exp(s - m)
                denom = jnp.sum(p, axis=-1, keepdims=True) + jnp.exp(sink - m)
                probs.append((p / denom).astype(bf16))
            return jnp.concatenate(probs, axis=0)

        def tail(p_stack):
            o = _mm(p_stack, vv_ref[g, krows, :])
            res = None
            for j in range(Q_PER_KV // 2):
                even = o[(2 * j) * WINDOW:(2 * j + 1) * WINDOW, :]
                odd = o[(2 * j + 1) * WINDOW:(2 * j + 2) * WINDOW, :]
                grp = g * (Q_PER_KV // 2) + j
                cols = slice(grp * LANES, (grp + 1) * LANES)
                res = jnp.where(low_half, even, odd) * _silu(cg_ref[rows, cols])
                y_ref[rows, 2 * GROUP_W + grp * LANES:2 * GROUP_W + (grp + 1) * LANES] = res.astype(bf16)
            return res[0:SUBLANES, :]

        return head, tail

    def glu(token):
        a = _value_after(wide_ref[0:n_conv, 0:GROUP_W], token)
        res = jnp.where(conv_valid, a * jax.nn.sigmoid(wide_ref[0:n_conv, GROUP_W:2 * GROUP_W]), 0.0)
        cbuf_ref[...] = res
        return res[0:SUBLANES, 0:LANES]

    def conv(r0, lg):
        def piece(token):
            cols = slice(lg * LANES, (lg + 1) * LANES)
            res = _depthwise_conv(cbuf_ref, cm_conv_w_ref, CONV_WIDTH, CONV_HALO - CONV_HALF, r0, CONV_ROWS,
                                  cols, token)
            convout_ref[r0:r0 + CONV_ROWS, cols] = res
            return res[0:SUBLANES, :]
        return piece

    def conv_norm(r0):
        def piece(token):
            rows = slice(r0, r0 + CONV_ROWS)
            c = _value_after(convout_ref[rows, :], token) + cm_conv_b_ref[...]
            b_act = _silu(_layernorm(c, cm_ln_g_ref[...], cm_ln_b_ref[...]))
            res = b_act * _silu(bgate_ref[rows, :])
            y_ref[rows, GROUP_W:2 * GROUP_W] = res.astype(bf16)
            return res[0:SUBLANES, 0:LANES]
        return piece

    def sgu_norm(token):
        v = _value_after(ha_ref[:, GROUP_W:2 * GROUP_W], token)
        res = _layernorm(v, sgu_ln_g_ref[...], sgu_ln_b_ref[...])
        av_ref[...] = res.astype(bf16)
        return res[0:SUBLANES, 0:LANES]

    def sgu_mix(c):
        def piece(token):
            rows = slice(c * CHUNK, (c + 1) * CHUNK)
            res = None
            for h in range(A_HEADS):
                cols = slice(h * LANES, (h + 1) * LANES)
                mixed = _mm(sgu_w_ref[h], av_ref[rows, cols]) + sgu_b_ref[h]
                a_u = ha_ref[rows, cols]
                if h == 0:
                    a_u = _value_after(a_u, token)
                a_gate = _silu(ha_ref[rows, 2 * GROUP_W + h * LANES:2 * GROUP_W + (h + 1) * LANES])
                res = a_u * mixed * a_gate
                y_ref[rows, cols] = res.astype(bf16)
            return res[0:SUBLANES, :]
        return piece

    def short_fill(token):
        dc = _value_after(wide_ref[0:n_conv, 0:GROUP_W], token)
        res = jnp.where(conv_valid, dc * wide_ref[0:n_conv, GROUP_W:2 * GROUP_W], 0.0)
        dbuf_ref[...] = res
        return res[0:SUBLANES, 0:LANES]

    def store_to(ref, rows=None, cols=slice(None), scale=None):
        def store(r):
            if scale is not None:
                r = r * scale
            if rows is None:
                ref[:, cols] = r.astype(ref.dtype)
            else:
                ref[rows, cols] = r.astype(ref.dtype)
        return store

    assert tile == 2 * WINDOW and tile == 4 * CONV_ROWS and tile == 2 * CHUNK
    cv = [conv(r0, lg) for r0 in range(0, tile, CONV_ROWS) for lg in range(GROUP_W // LANES)]
    nm = [conv_norm(r0) for r0 in range(0, tile, CONV_ROWS)]
    unit(store_to(q_ref, scale=ATT_HEAD_DIM ** -0.5), main, OFF_C, GROUP_W)
    unit(store_to(wide_ref, slice(0, tile + 2 * HALO), slice(0, 2 * KV_W)), kv_rows, OFF_CK, 2 * KV_W)
    unit(store_to(cg_ref), main, OFF_CG, GROUP_W, [kv_prep])
    unit(store_to(wide_ref, slice(0, n_conv)), conv_rows, OFF_B, 2 * GROUP_W, [attention(0, 0), attention(0, 1)])
    unit(store_to(bgate_ref), main, OFF_B + 2 * GROUP_W, GROUP_W, [glu, attention(1, 0)])
    unit(store_to(ha_ref, cols=slice(0, GROUP_W)), main, OFF_A, GROUP_W, [attention(1, 1)] + cv[0:2])
    unit(store_to(ha_ref, cols=slice(GROUP_W, 2 * GROUP_W)), main, OFF_A + GROUP_W, GROUP_W, cv[2:6])
    unit(store_to(ha_ref, cols=slice(2 * GROUP_W, 3 * GROUP_W)), main, OFF_A + 2 * GROUP_W, GROUP_W,
         cv[6:8] + [nm[0], sgu_norm])
    unit(store_to(db_ref), main, OFF_D, GROUP_W, cv[8:12])
    unit(store_to(wide_ref, slice(0, n_conv)), conv_rows, OFF_D + GROUP_W, 2 * GROUP_W,
         [nm[1]] + cv[12:16] + [nm[2], sgu_mix(0), sgu_mix(1)])
    unit(store_to(dg_ref), main, OFF_D + 3 * GROUP_W, GROUP_W, [nm[3], short_fill])

    sconv = _depthwise_conv(dbuf_ref, sc_conv_w_ref, SHORT_CONV_WIDTH,
                            CONV_HALO - SHORT_CONV_WIDTH // 2, 0, tile)
    y_ref[:, 3 * GROUP_W:4 * GROUP_W] = (db_ref[...] * sconv * _silu(dg_ref[...])).astype(bf16)

    out = _mm(y_ref[...], w_out_ref[...])
    out_ref[0] = _layernorm(DEEPNORM_ALPHA * xm_ref[0] + out, post_ln_g_ref[...], post_ln_b_ref[...])


def _layer_params(layer, shape):
    return pl.BlockSpec((None,) + shape, lambda b, i: (layer,) + (0,) * len(shape),
                        pipeline_mode=pl.Buffered(1))


def _layer(layer, x, sink, w_in, w_out, sgu_ln_g, sgu_ln_b, sgu_w, sgu_b, cm_conv_w, cm_conv_b,
           cm_ln_g, cm_ln_b, sc_conv_w, post_ln_g, post_ln_b):
    _resident = functools.partial(_layer_params, layer)
    batch, seq_len, d = x.shape
    tile = TILE
    assert d == D_MODEL and seq_len % tile == 0 and tile % HALO == 0
    halo_per_tile = tile // HALO
    n_halo_blocks = seq_len // HALO

    in_specs = [
        pl.BlockSpec(memory_space=pltpu.SMEM),
        pl.BlockSpec((1, HALO, d), lambda b, i: (b, jnp.maximum(i * halo_per_tile - 1, 0), 0)),
        pl.BlockSpec((1, tile, d), lambda b, i: (b, i, 0)),
        pl.BlockSpec((1, HALO, d), lambda b, i: (b, jnp.minimum((i + 1) * halo_per_tile, n_halo_blocks - 1), 0)),
        _resident((D_MODEL, D_IN)),
        _resident((D_MODEL, D_MODEL)),
        _resident((1, GROUP_W)), _resident((1, GROUP_W)),
        _resident((A_HEADS, CHUNK, CHUNK)), _resident((A_HEADS, CHUNK, 1)),
        _resident((CONV_WIDTH, GROUP_W)), _resident((1, GROUP_W)),
        _resident((1, GROUP_W)), _resident((1, GROUP_W)),
        _resident((SHORT_CONV_WIDTH, GROUP_W)),
        _resident((1, D_MODEL)), _resident((1, D_MODEL)),
    ]
    scratch = [
        pltpu.VMEM((tile + 2 * HALO, D_MODEL), jnp.bfloat16),
        pltpu.VMEM((tile, D_MODEL), jnp.bfloat16),
        pltpu.VMEM((tile + 2 * CONV_HALO, GROUP_W), jnp.float32),
        pltpu.VMEM((tile + 2 * CONV_HALO, GROUP_W), jnp.float32),
        pltpu.VMEM((tile, GROUP_W), jnp.float32),
        pltpu.VMEM((tile + 2 * HALO, 2 * GROUP_W), jnp.float32),
        pltpu.VMEM((tile, 3 * GROUP_W), jnp.float32),
        pltpu.VMEM((tile, GROUP_W), jnp.bfloat16),
        pltpu.VMEM((tile, GROUP_W), jnp.float32),
        pltpu.VMEM((tile, GROUP_W), jnp.bfloat16),
        pltpu.VMEM((KV_HEADS, tile + 2 * HALO, LANES), jnp.bfloat16),
        pltpu.VMEM((KV_HEADS, tile + 2 * HALO, LANES), jnp.bfloat16),
        pltpu.VMEM((tile, GROUP_W), jnp.float32),
        pltpu.VMEM((tile, GROUP_W), jnp.float32),
        pltpu.VMEM((tile, GROUP_W), jnp.float32),
    ]
    return pl.pallas_call(
        functools.partial(_layer_kernel, layer=layer, seq_len=seq_len),
        grid=(batch, seq_len // tile),
        in_specs=in_specs,
        out_specs=pl.BlockSpec((1, tile, d), lambda b, i: (b, i, 0)),
        out_shape=jax.ShapeDtypeStruct(x.shape, x.dtype),
        scratch_shapes=scratch,
        compiler_params=pltpu.CompilerParams(
            dimension_semantics=("arbitrary", "arbitrary"),
            vmem_limit_bytes=V7X_VMEM_BYTES - 4 * 1024 * 1024),
        name="encoder_layer",
    )(sink, x, x, x, w_in, w_out, sgu_ln_g, sgu_ln_b, sgu_w, sgu_b, cm_conv_w, cm_conv_b,
      cm_ln_g, cm_ln_b, sc_conv_w, post_ln_g, post_ln_b)


def kernel(x_prompt, x_sample, w_in, w_out, sgu_ln_g, sgu_ln_b, sgu_w, sgu_b, cm_conv_w, cm_conv_b,
           cm_ln_g, cm_ln_b, attn_sink, sc_conv_w, post_ln_g, post_ln_b):
    bf16 = jnp.bfloat16
    row = lambda a: a[:, None, :]
    params = (attn_sink, w_in.astype(bf16), w_out.astype(bf16), row(sgu_ln_g), row(sgu_ln_b),
              sgu_w.astype(bf16), sgu_b[:, :, :, None], cm_conv_w, row(cm_conv_b),
              row(cm_ln_g), row(cm_ln_b), sc_conv_w, row(post_ln_g), row(post_ln_b))

    def trunk(x):
        for layer in range(DEPTH):
            x = _layer(layer, x, *params)
        return x

    return (trunk(x_prompt), trunk(x_sample))
```

```python
import functools
import math

import jax
import jax.numpy as jnp
from jax import lax
from jax.experimental import pallas as pl
from jax.experimental.pallas import tpu as pltpu

D_MODEL = 2048
DEPTH = 4
GROUP_W = 512
CHUNK = 128
A_HEADS = 4
CONV_WIDTH = 31
CONV_HALF = CONV_WIDTH // 2
SHORT_CONV_WIDTH = 3
ATT_HEAD_DIM = 64
Q_HEADS = 8
KV_HEADS = 2
Q_PER_KV = Q_HEADS // KV_HEADS
WINDOW = 128
KV_W = KV_HEADS * ATT_HEAD_DIM
D_IN = 6400
DEEPNORM_ALPHA = (2.0 * DEPTH) ** 0.25
LN_EPS = 1e-5
NEG_BIG = -1e30

OFF_A = 0
OFF_B = 3 * GROUP_W
OFF_C = 6 * GROUP_W
OFF_CK = OFF_C + GROUP_W
OFF_CG = OFF_CK + 2 * KV_W
OFF_D = OFF_CG + GROUP_W
assert OFF_D + 4 * GROUP_W == D_IN

LANES = 128
SUBLANES = 8
BF16_SUBLANES = 16
CONV_ROWS = 64
V7X_VMEM_BYTES = 64 * 1024 * 1024

TILE = 256
HALO = WINDOW
CONV_HALO = BF16_SUBLANES
assert CONV_HALO >= CONV_HALF and HALO % CONV_HALO == 0


def _layernorm(x, g, b):
    mu = jnp.mean(x, axis=-1, keepdims=True)
    xc = x - mu
    var = jnp.mean(xc * xc, axis=-1, keepdims=True)
    return xc * lax.rsqrt(var + LN_EPS) * g + b


def _silu(x):
    return x * jax.nn.sigmoid(x)


def _mm(a, b):
    return jnp.dot(a, b, preferred_element_type=jnp.float32)


def _dep_zero(tile):
    u = lax.bitcast_convert_type(tile, jnp.uint32)
    return lax.shift_right_logical(lax.shift_right_logical(u, jnp.uint32(16)), jnp.uint32(16))


def _after(tile, *deps):
    u = lax.bitcast_convert_type(tile, jnp.uint32)
    for d in deps:
        u = u | _dep_zero(d)
    return lax.bitcast_convert_type(u, jnp.float32)


def _value_after(x, token):
    if token is None:
        return x
    head = x[0:SUBLANES, :]
    first = _after(head[:, 0:LANES], token)
    if head.shape[1] > LANES:
        first = jnp.concatenate([first, head[:, LANES:]], axis=1)
    if x.shape[0] == SUBLANES:
        return first
    return jnp.concatenate([first, x[SUBLANES:, :]], axis=0)


def _depthwise_conv(buf_ref, w_ref, n_taps, first_row, row0, n_rows, cols=slice(None), start_after=None):
    out = None
    for s in range(SUBLANES):
        taps = [k for k in range(n_taps) if (first_row + k) % SUBLANES == s]
        if not taps:
            continue
        part = None
        for k in taps:
            a = row0 + first_row + k - s
            blk = buf_ref[a:a + n_rows + SUBLANES, cols]
            if part is None and out is None:
                blk = _value_after(blk, start_after)
            term = blk * w_ref[k:k + 1, cols]
            part = term if part is None else part + term
        shifted = part[s:s + n_rows, :]
        out = shifted if out is None else out + shifted
    return out


def _layer_kernel(sink_ref, xm_ref, xr_ref, w_in_ref, w_out_ref,
                  sgu_ln_g_ref, sgu_ln_b_ref, sgu_w_ref, sgu_b_ref,
                  cm_conv_w_ref, cm_conv_b_ref, cm_ln_g_ref, cm_ln_b_ref,
                  sc_conv_w_ref, post_ln_g_ref, post_ln_b_ref,
                  out_ref, xb_ref, y_ref, cbuf_ref, dbuf_ref, convout_ref, wide_ref, ha_ref, av_ref,
                  bgate_ref, q_ref, kk_ref, vt_ref, cg_ref, db_ref, dg_ref, *, layer, seq_len):
    tile = xm_ref.shape[1]
    i = pl.program_id(1)
    t0 = i * tile
    bf16 = jnp.bfloat16
    first_of_seq = i == 0

    @pl.when((pl.program_id(0) == 0) & first_of_seq)
    def _():
        kk_ref[...] = jnp.zeros_like(kk_ref)
        vt_ref[...] = jnp.zeros_like(vt_ref)
        cbuf_ref[...] = jnp.zeros_like(cbuf_ref)
        dbuf_ref[...] = jnp.zeros_like(dbuf_ref)

    xb_ref[0:tile, :] = xm_ref[0].astype(bf16)
    xb_ref[tile:, :] = xr_ref[0].astype(bf16)

    main = slice(0, tile)
    conv_rows = slice(0, tile + CONV_HALO)
    kv_rows = slice(0, tile + HALO)
    n_conv = tile + 2 * CONV_HALO
    n_new = tile + CONV_HALO

    cpos = t0 + lax.broadcasted_iota(jnp.int32, (n_new, 1), 0)
    conv_valid = cpos < seq_len
    lane = lax.broadcasted_iota(jnp.int32, (1, LANES), 1)
    low_half = lane < ATT_HEAD_DIM

    state = {"token": None}

    def unit(store, rows, off, width, pieces=()):
        started = [p[0](state["token"]) if isinstance(p, tuple) else p(state["token"]) for p in pieces]
        r = _mm(xb_ref[rows, :], w_in_ref[:, off:off + width])
        done = [p[1](st) if isinstance(p, tuple) else st for p, st in zip(pieces, started)]
        head = r[0:SUBLANES, 0:LANES]
        if done:
            head = _after(head, *done)
            first = jnp.concatenate([head, r[0:SUBLANES, LANES:]], axis=1)
            r = jnp.concatenate([first, r[SUBLANES:, :]], axis=0)
        store(r)
        state["token"] = r[SUBLANES:2 * SUBLANES, 0:LANES]

    def kv_prep(token):
        for g in range(KV_HEADS):
            kk_ref[g, 0:HALO, :] = kk_ref[g, tile:tile + HALO, :]
            vt_ref[g, :, 0:HALO] = vt_ref[g, :, tile:tile + HALO]
        k_all = _value_after(wide_ref[0:tile + HALO, 0:KV_W], token)
        v_all = wide_ref[0:tile + HALO, KV_W:2 * KV_W]
        k_sw = pltpu.roll(k_all, ATT_HEAD_DIM, 1)
        v_sw = pltpu.roll(v_all, ATT_HEAD_DIM, 1)
        kk_ref[0, HALO:, :] = jnp.where(low_half, k_all, k_sw).astype(bf16)
        kk_ref[1, HALO:, :] = jnp.where(low_half, k_sw, k_all).astype(bf16)
        vt_ref[0, :, HALO:] = jnp.where(low_half, v_all, v_sw).T.astype(bf16)
        vt_ref[1, :, HALO:] = jnp.where(low_half, v_sw, v_all).T.astype(bf16)
        return k_sw[0:SUBLANES, :]

    n_keys = 3 * WINDOW

    def attention(n, g):
        rows = slice(n * WINDOW, (n + 1) * WINDOW)
        krows = slice(n * WINDOW, n * WINDOW + n_keys)

        def head(token):
            ki = lax.broadcasted_iota(jnp.int32, (n_keys, WINDOW), 0)
            qj = lax.broadcasted_iota(jnp.int32, (n_keys, WINDOW), 1)
            dist = jnp.abs(qj - ki + WINDOW)
            dist_f = dist.astype(jnp.float32)
            kpos = t0 + (n - 1) * WINDOW + ki
            valid = (dist <= WINDOW) & (kpos >= 0) & (kpos < seq_len)
            stacked = []
            for r in range(Q_PER_KV):
                hd = g * Q_PER_KV + r
                q_pair = q_ref[rows, (hd // 2) * LANES:(hd // 2 + 1) * LANES]
                keep = low_half if hd % 2 == 0 else ~low_half
                stacked.append(jnp.where(keep, q_pair, jnp.zeros((), bf16)))
            q_stack = jnp.concatenate(stacked, axis=0)
            scores = lax.dot_general(kk_ref[g, krows, :], q_stack, (((1,), (1,)), ((), ())),
                                     preferred_element_type=jnp.float32)
            probs = []
            for r in range(Q_PER_KV):
                hd = g * Q_PER_KV + r
                slope = 2.0 ** (-(8.0 / Q_HEADS) * (hd + 1.0))
                sink = sink_ref[layer, hd]
                s = scores[:, r * WINDOW:(r + 1) * WINDOW]
                if r == 0:
                    s = _value_after(s, token)
                s = jnp.where(valid, s - slope * dist_f, NEG_BIG)
                m = jnp.maximum(jnp.max(s, axis=0, keepdims=True), sink)
                p = jnp.exp(s - m)
                denom = jnp.sum(p, axis=0, keepdims=True) + jnp.exp(sink - m)
                probs.append((p / denom).astype(bf16))
            return jnp.concatenate(probs, axis=1)

        def tail(p_t):
            o_t = _mm(vt_ref[g, :, krows], p_t)
            upper = lax.broadcasted_iota(jnp.int32, (LANES, 1), 0) < ATT_HEAD_DIM
            res = None
            for j in range(Q_PER_KV // 2):
                even = o_t[:, (2 * j) * WINDOW:(2 * j + 1) * WINDOW]
                odd = o_t[:, (2 * j + 1) * WINDOW:(2 * j + 2) * WINDOW]
                grp = g * (Q_PER_KV // 2) + j
                cols = slice(grp * LANES, (grp + 1) * LANES)
                res = jnp.where(upper, even, odd).T * _silu(cg_ref[rows, cols])
                y_ref[rows, 2 * GROUP_W + grp * LANES:2 * GROUP_W + (grp + 1) * LANES] = res.astype(bf16)
            return res[0:SUBLANES, :]

        return head, tail

    def carry_conv_halo(buf_ref):
        buf_ref[0:CONV_HALO, :] = jnp.where(first_of_seq, 0.0, buf_ref[tile:tile + CONV_HALO, :])

    def glu(token):
        carry_conv_halo(cbuf_ref)
        a = _value_after(wide_ref[0:n_new, 0:GROUP_W], token)
        res = jnp.where(conv_valid, a * jax.nn.sigmoid(wide_ref[0:n_new, GROUP_W:2 * GROUP_W]), 0.0)
        cbuf_ref[CONV_HALO:, :] = res
        return res[0:SUBLANES, 0:LANES]

    def conv(r0, lg):
        def piece(token):
            cols = slice(lg * LANES, (lg + 1) * LANES)
            res = _depthwise_conv(cbuf_ref, cm_conv_w_ref, CONV_WIDTH, CONV_HALO - CONV_HALF, r0, CONV_ROWS,
                                  cols, token)
            convout_ref[r0:r0 + CONV_ROWS, cols] = res
            return res[0:SUBLANES, :]
        return piece

    def conv_norm(r0):
        def piece(token):
            rows = slice(r0, r0 + CONV_ROWS)
            c = _value_after(convout_ref[rows, :], token) + cm_conv_b_ref[...]
            b_act = _silu(_layernorm(c, cm_ln_g_ref[...], cm_ln_b_ref[...]))
            res = b_act * _silu(bgate_ref[rows, :])
            y_ref[rows, GROUP_W:2 * GROUP_W] = res.astype(bf16)
            return res[0:SUBLANES, 0:LANES]
        return piece

    def sgu_norm(token):
        v = _value_after(ha_ref[:, GROUP_W:2 * GROUP_W], token)
        res = _layernorm(v, sgu_ln_g_ref[...], sgu_ln_b_ref[...])
        av_ref[...] = res.astype(bf16)
        return res[0:SUBLANES, 0:LANES]

    def sgu_mix(c):
        def piece(token):
            rows = slice(c * CHUNK, (c + 1) * CHUNK)
            res = None
            for h in range(A_HEADS):
                cols = slice(h * LANES, (h + 1) * LANES)
                mixed = _mm(sgu_w_ref[h], av_ref[rows, cols]) + sgu_b_ref[h]
                a_u = ha_ref[rows, cols]
                if h == 0:
                    a_u = _value_after(a_u, token)
                a_gate = _silu(ha_ref[rows, 2 * GROUP_W + h * LANES:2 * GROUP_W + (h + 1) * LANES])
                res = a_u * mixed * a_gate
                y_ref[rows, cols] = res.astype(bf16)
            return res[0:SUBLANES, :]
        return piece

    def short_fill(token):
        carry_conv_halo(dbuf_ref)
        dc = _value_after(wide_ref[0:n_new, 0:GROUP_W], token)
        res = jnp.where(conv_valid, dc * wide_ref[0:n_new, GROUP_W:2 * GROUP_W], 0.0)
        dbuf_ref[CONV_HALO:, :] = res
        return res[0:SUBLANES, 0:LANES]

    def store_to(ref, rows=None, cols=slice(None), scale=None):
        def store(r):
            if scale is not None:
                r = r * scale
            if rows is None:
                ref[:, cols] = r.astype(ref.dtype)
            else:
                ref[rows, cols] = r.astype(ref.dtype)
        return store

    assert tile == 2 * WINDOW and tile == 4 * CONV_ROWS and tile == 2 * CHUNK
    cv = [conv(r0, lg) for r0 in range(0, tile, CONV_ROWS) for lg in range(GROUP_W // LANES)]
    nm = [conv_norm(r0) for r0 in range(0, tile, CONV_ROWS)]
    unit(store_to(q_ref, scale=ATT_HEAD_DIM ** -0.5), main, OFF_C, GROUP_W)
    unit(store_to(wide_ref, slice(0, tile + HALO), slice(0, 2 * KV_W)), kv_rows, OFF_CK, 2 * KV_W)
    unit(store_to(cg_ref), main, OFF_CG, GROUP_W, [kv_prep])
    unit(store_to(wide_ref, slice(0, n_new)), conv_rows, OFF_B, 2 * GROUP_W, [attention(0, 0), attention(0, 1)])
    unit(store_to(bgate_ref), main, OFF_B + 2 * GROUP_W, GROUP_W, [glu, attention(1, 0)])
    unit(store_to(ha_ref, cols=slice(0, GROUP_W)), main, OFF_A, GROUP_W, [attention(1, 1)] + cv[0:2])
    unit(store_to(ha_ref, cols=slice(GROUP_W, 2 * GROUP_W)), main, OFF_A + GROUP_W, GROUP_W, cv[2:6])
    unit(store_to(ha_ref, cols=slice(2 * GROUP_W, 3 * GROUP_W)), main, OFF_A + 2 * GROUP_W, GROUP_W,
         cv[6:8] + [nm[0], sgu_norm])
    unit(store_to(db_ref), main, OFF_D, GROUP_W, cv[8:12])
    unit(store_to(wide_ref, slice(0, n_new)), conv_rows, OFF_D + GROUP_W, 2 * GROUP_W,
         [nm[1]] + cv[12:16] + [nm[2], sgu_mix(0), sgu_mix(1)])
    unit(store_to(dg_ref), main, OFF_D + 3 * GROUP_W, GROUP_W, [nm[3], short_fill])

    sconv = _depthwise_conv(dbuf_ref, sc_conv_w_ref, SHORT_CONV_WIDTH,
                            CONV_HALO - SHORT_CONV_WIDTH // 2, 0, tile)
    y_ref[:, 3 * GROUP_W:4 * GROUP_W] = (db_ref[...] * sconv * _silu(dg_ref[...])).astype(bf16)

    out = _mm(y_ref[...], w_out_ref[...])
    out_ref[0] = _layernorm(DEEPNORM_ALPHA * xm_ref[0] + out, post_ln_g_ref[...], post_ln_b_ref[...])


def _layer_params(layer, shape):
    return pl.BlockSpec((None,) + shape, lambda b, i: (layer,) + (0,) * len(shape),
                        pipeline_mode=pl.Buffered(1))


def _layer(layer, x, sink, w_in, w_out, sgu_ln_g, sgu_ln_b, sgu_w, sgu_b, cm_conv_w, cm_conv_b,
           cm_ln_g, cm_ln_b, sc_conv_w, post_ln_g, post_ln_b):
    _resident = functools.partial(_layer_params, layer)
    batch, seq_len, d = x.shape
    tile = TILE
    assert d == D_MODEL and seq_len % tile == 0 and tile % HALO == 0
    halo_per_tile = tile // HALO
    n_halo_blocks = seq_len // HALO

    in_specs = [
        pl.BlockSpec(memory_space=pltpu.SMEM),
        pl.BlockSpec((1, tile, d), lambda b, i: (b, i, 0)),
        pl.BlockSpec((1, HALO, d), lambda b, i: (b, jnp.minimum((i + 1) * halo_per_tile, n_halo_blocks - 1), 0)),
        _resident((D_MODEL, D_IN)),
        _resident((D_MODEL, D_MODEL)),
        _resident((1, GROUP_W)), _resident((1, GROUP_W)),
        _resident((A_HEADS, CHUNK, CHUNK)), _resident((A_HEADS, CHUNK, 1)),
        _resident((CONV_WIDTH, GROUP_W)), _resident((1, GROUP_W)),
        _resident((1, GROUP_W)), _resident((1, GROUP_W)),
        _resident((SHORT_CONV_WIDTH, GROUP_W)),
        _resident((1, D_MODEL)), _resident((1, D_MODEL)),
    ]
    scratch = [
        pltpu.VMEM((tile + HALO, D_MODEL), jnp.bfloat16),
        pltpu.VMEM((tile, D_MODEL), jnp.bfloat16),
        pltpu.VMEM((tile + 2 * CONV_HALO, GROUP_W), jnp.float32),
        pltpu.VMEM((tile + 2 * CONV_HALO, GROUP_W), jnp.float32),
        pltpu.VMEM((tile, GROUP_W), jnp.float32),
        pltpu.VMEM((tile + HALO, 2 * GROUP_W), jnp.float32),
        pltpu.VMEM((tile, 3 * GROUP_W), jnp.float32),
        pltpu.VMEM((tile, GROUP_W), jnp.bfloat16),
        pltpu.VMEM((tile, GROUP_W), jnp.float32),
        pltpu.VMEM((tile, GROUP_W), jnp.bfloat16),
        pltpu.VMEM((KV_HEADS, tile + 2 * HALO, LANES), jnp.bfloat16),
        pltpu.VMEM((KV_HEADS, LANES, tile + 2 * HALO), jnp.bfloat16),
        pltpu.VMEM((tile, GROUP_W), jnp.float32),
        pltpu.VMEM((tile, GROUP_W), jnp.float32),
        pltpu.VMEM((tile, GROUP_W), jnp.float32),
    ]
    return pl.pallas_call(
        functools.partial(_layer_kernel, layer=layer, seq_len=seq_len),
        grid=(batch, seq_len // tile),
        in_specs=in_specs,
        out_specs=pl.BlockSpec((1, tile, d), lambda b, i: (b, i, 0)),
        out_shape=jax.ShapeDtypeStruct(x.shape, x.dtype),
        scratch_shapes=scratch,
        compiler_params=pltpu.CompilerParams(
            dimension_semantics=("arbitrary", "arbitrary"),
            vmem_limit_bytes=V7X_VMEM_BYTES - 4 * 1024 * 1024),
        name="encoder_layer",
    )(sink, x, x, w_in, w_out, sgu_ln_g, sgu_ln_b, sgu_w, sgu_b, cm_conv_w, cm_conv_b,
      cm_ln_g, cm_ln_b, sc_conv_w, post_ln_g, post_ln_b)


def kernel(x_prompt, x_sample, w_in, w_out, sgu_ln_g, sgu_ln_b, sgu_w, sgu_b, cm_conv_w, cm_conv_b,
           cm_ln_g, cm_ln_b, attn_sink, sc_conv_w, post_ln_g, post_ln_b):
    bf16 = jnp.bfloat16
    row = lambda a: a[:, None, :]
    params = (attn_sink, w_in.astype(bf16), w_out.astype(bf16), row(sgu_ln_g), row(sgu_ln_b),
              sgu_w.astype(bf16), sgu_b[:, :, :, None], cm_conv_w, row(cm_conv_b),
              row(cm_ln_g), row(cm_ln_b), sc_conv_w, row(post_ln_g), row(post_ln_b))

    def trunk(x):
        for layer in range(DEPTH):
            x = _layer(layer, x, *params)
        return x

    return (trunk(x_prompt), trunk(x_sample))
```
